```python
import math
import jax
import jax.numpy as jnp
from jax import lax
import numpy as np

D_MODEL = 2048
BATCH = 8
SEQ = 2048
DEPTH = 1

MEM_LEN = 256
RMS_EPS = 1e-6
DN_HEAD_DIM = 128
DN_WIDTH = D_MODEL // 2
DN_HEADS = DN_WIDTH // DN_HEAD_DIM
DN_CONV = 4
DN_CHUNK = 64
RW_HEAD_DIM = 64
RW_WIDTH = D_MODEL - DN_WIDTH
RW_HEADS = RW_WIDTH // RW_HEAD_DIM
RW_DECAY_LORA = 64
RW_AAA_LORA = 64
RW_GATE_LORA = 128
RW_GN_EPS = 64e-5
DN_COLS = 4 * DN_WIDTH + 2 * DN_HEADS
RW_COLS = 3 * RW_WIDTH + RW_DECAY_LORA + RW_AAA_LORA + RW_GATE_LORA
IN_COLS = DN_COLS + RW_COLS
XA_HEADS = 4
XA_HEAD_DIM = 128
XA_WIDTH = XA_HEADS * XA_HEAD_DIM
FFN_HIDDEN = 4 * D_MODEL

kernel_name = 'hybrid_gdn_rwkv7_memxattn_layer'


def rmsnorm(x, w):
    xf = x.astype(jnp.float32)
    y = xf * lax.rsqrt(jnp.mean(xf * xf, axis=-1, keepdims=True) + RMS_EPS)
    return (y * w.astype(jnp.float32)).astype(x.dtype)


def l2norm(x):
    return x * lax.rsqrt(jnp.sum(x * x, axis=-1, keepdims=True) + 1e-6)


def causal_depthwise_conv(x, w):
    k = w.shape[0]
    return lax.conv_general_dilated(
        x, w[:, None, :].astype(x.dtype), window_strides=(1,), padding=[(k - 1, 0)],
        dimension_numbers=('NWC', 'WIO', 'NWC'), feature_group_count=x.shape[-1])


def token_shift(p):
    return jnp.pad(p, ((0, 0), (1, 0), (0, 0)))[:, :-1]


def chunked_gated_delta_rule(q, k, v, g, beta):
    b, s, h, dk = q.shape
    dv = v.shape[-1]
    c = DN_CHUNK
    n = s // c

    def chunks(t):
        return t.reshape(b, n, c, h, -1).transpose(0, 3, 1, 2, 4)

    q = chunks(q) * (dk ** -0.5)
    k = chunks(k)
    v = chunks(v)
    g = g.reshape(b, n, c, h).transpose(0, 3, 1, 2)
    beta = beta.reshape(b, n, c, h).transpose(0, 3, 1, 2)
    gc = jnp.cumsum(g, axis=-1)
    idx = jnp.arange(c)
    causal = idx[:, None] >= idx[None, :]
    strict = idx[:, None] > idx[None, :]
    decay = jnp.exp(jnp.where(causal, gc[..., :, None] - gc[..., None, :], -jnp.inf))
    kb = k * beta[..., None]
    a_strict = jnp.where(strict, jnp.einsum('bhncd,bhnmd->bhncm', kb, k) * decay, 0.0)
    lower = a_strict + jnp.eye(c, dtype=q.dtype)
    rhs = jnp.concatenate([v * beta[..., None], kb * jnp.exp(gc)[..., None]], axis=-1)
    sol = lax.linalg.triangular_solve(lower, rhs, left_side=True, lower=True)
    u, w = sol[..., :dv], sol[..., dv:]
    attn = jnp.einsum('bhncd,bhnmd->bhncm', q, k) * decay
    qg = q * jnp.exp(gc)[..., None]
    g_last = gc[..., -1]
    kd = k * jnp.exp(g_last[..., None] - gc)[..., None]
    xs = tuple(jnp.moveaxis(t, 2, 0) for t in (u, w, qg, attn, kd, g_last))

    def step(state, xs_n):
        u_n, w_n, qg_n, attn_n, kd_n, gl_n = xs_n
        v_new = u_n - jnp.einsum('bhcd,bhde->bhce', w_n, state)
        o_n = jnp.einsum('bhcd,bhde->bhce', qg_n, state) + jnp.einsum('bhcm,bhme->bhce', attn_n, v_new)
        state = state * jnp.exp(gl_n)[..., None, None] + jnp.einsum('bhcd,bhce->bhde', kd_n, v_new)
        return state, o_n

    s0 = jnp.zeros((b, h, dk, dv), q.dtype)
    _, o = lax.scan(step, s0, xs)
    return o.transpose(1, 0, 3, 2, 4).reshape(b, s, h, dv)


def wkv7_scan(r, w, k, v, kk, a):
    b, s, h, nd = r.shape
    xs = tuple(jnp.swapaxes(t, 0, 1) for t in (r, w, k, v, kk, a))

    def step(state, xs_t):
        r_t, w_t, k_t, v_t, kk_t, a_t = xs_t
        sa = jnp.einsum('bhvk,bhk->bhv', state, -kk_t)
        state = (state * w_t[:, :, None, :] + sa[..., None] * (kk_t * a_t)[:, :, None, :]
                 + v_t[..., None] * k_t[:, :, None, :])
        return state, jnp.einsum('bhvk,bhk->bhv', state, r_t)

    s0 = jnp.zeros((b, h, nd, nd), r.dtype)
    _, y = lax.scan(step, s0, xs)
    return jnp.swapaxes(y, 0, 1)


def deltanet_group(p, conv_w, a_log, dt_bias, norm_w):
    p = p.astype(jnp.float32)
    b, s, _ = p.shape
    qkv, z, ga, gb = jnp.split(p, [3 * DN_WIDTH, 4 * DN_WIDTH, 4 * DN_WIDTH + DN_HEADS], axis=-1)
    qkv = jax.nn.silu(causal_depthwise_conv(qkv, conv_w))
    q, k, v = (t.reshape(b, s, DN_HEADS, DN_HEAD_DIM) for t in jnp.split(qkv, 3, axis=-1))
    q = l2norm(q)
    k = l2norm(k)
    beta = jax.nn.sigmoid(gb)
    g = -jnp.exp(a_log) * jax.nn.softplus(ga + dt_bias)
    o = chunked_gated_delta_rule(q, k, v, g, beta)
    o = rmsnorm(o, norm_w) * jax.nn.silu(z.reshape(b, s, DN_HEADS, DN_HEAD_DIM))
    return o.reshape(b, s, DN_WIDTH)


def rwkv7_group(p, mu, w0, w2, a0, a2, g2, k_k, k_a, r_k, ln_w, ln_b):
    p = p.astype(jnp.float32)
    b, s, _ = p.shape
    p = p + (token_shift(p) - p) * mu
    pr, pk, pv, pw, pa, pg = jnp.split(
        p, [RW_WIDTH, 2 * RW_WIDTH, 3 * RW_WIDTH, 3 * RW_WIDTH + RW_DECAY_LORA,
            3 * RW_WIDTH + RW_DECAY_LORA + RW_AAA_LORA], axis=-1)

    def heads(t):
        return t.reshape(b, s, RW_HEADS, RW_HEAD_DIM)

    log_w = -jax.nn.softplus(-(w0 + jnp.tanh(pw) @ w2)) - 0.5
    decay = jnp.exp(-jnp.exp(log_w))
    a = jax.nn.sigmoid(a0 + pa @ a2)
    gate = jax.nn.sigmoid(pg) @ g2
    kk = heads(pk * k_k)
    kk = kk / jnp.maximum(jnp.sqrt(jnp.sum(kk * kk, axis=-1, keepdims=True)), 1e-12)
    k = pk * (1.0 + (a - 1.0) * k_a)
    r_h, k_h, v_h = heads(pr), heads(k), heads(pv)
    y = wkv7_scan(r_h, heads(decay), k_h, v_h, kk, heads(a))
    mean = jnp.mean(y, axis=-1, keepdims=True)
    var = jnp.mean(jnp.square(y - mean), axis=-1, keepdims=True)
    y = ((y - mean) * lax.rsqrt(var + RW_GN_EPS)).reshape(b, s, RW_WIDTH) * ln_w + ln_b
    bonus = jnp.sum(r_h * k_h * r_k, axis=-1, keepdims=True) * v_h
    return (y + bonus.reshape(b, s, RW_WIDTH)) * gate


def memory_cross_attention(hn, mn, wq, wk, wv, wo):
    b, s, _ = hn.shape
    m = mn.shape[1]
    q = (hn @ wq).reshape(b, s, XA_HEADS, XA_HEAD_DIM)
    k = (mn @ wk).reshape(b, m, XA_HEADS, XA_HEAD_DIM)
    v = (mn @ wv).reshape(b, m, XA_HEADS, XA_HEAD_DIM)
    scores = jnp.einsum('bshd,bmhd->bhsm', q, k).astype(jnp.float32) * (XA_HEAD_DIM ** -0.5)
    probs = jax.nn.softmax(scores, axis=-1).astype(v.dtype)
    o = jnp.einsum('bhsm,bmhd->bshd', probs, v).reshape(b, s, XA_WIDTH)
    return o @ wo


def squared_relu_mlp(u, w1, w2):
    return jnp.square(jax.nn.relu(u @ w1)) @ w2


def setup_inputs(seed: int = 0) -> dict:
    key = jax.random.key(seed)
    ks = jax.random.split(key, 32)
    L = DEPTH
    D = D_MODEL

    def normal(i, shape, scale):
        return jax.random.normal(ks[i], shape, jnp.float32) * scale

    def uniform(i, shape, lo, hi):
        return jax.random.uniform(ks[i], shape, jnp.float32, lo, hi)

    dt = jnp.exp(uniform(6, (L, DN_HEADS), math.log(1e-3), math.log(1e-1)))
    return {
        'x': normal(0, (BATCH, SEQ, D), 1.0),
        'mem': normal(1, (BATCH, MEM_LEN, D), 1.0),
        'mix_norm_w': 1.0 + normal(2, (L, D), 0.02),
        'w_in': normal(3, (L, D, IN_COLS), D ** -0.5),
        'dn_conv_w': normal(4, (L, DN_CONV, 3 * DN_WIDTH), DN_CONV ** -0.5),
        'dn_a_log': jnp.log(uniform(5, (L, DN_HEADS), 1.0, 16.0)),
        'dn_dt_bias': dt + jnp.log(-jnp.expm1(-dt)),
        'dn_norm_w': 1.0 + normal(7, (L, DN_HEAD_DIM), 0.02),
        'rw_mu': uniform(8, (L, RW_COLS), 0.0, 1.0),
        'rw_w0': uniform(9, (L, RW_WIDTH), -6.0, 1.0),
        'rw_w2': normal(10, (L, RW_DECAY_LORA, RW_WIDTH), 0.1 * RW_DECAY_LORA ** -0.5),
        'rw_a0': normal(11, (L, RW_WIDTH), 0.1),
        'rw_a2': normal(12, (L, RW_AAA_LORA, RW_WIDTH), 0.1 * RW_AAA_LORA ** -0.5),
        'rw_g2': normal(13, (L, RW_GATE_LORA, RW_WIDTH), RW_GATE_LORA ** -0.5),
        'rw_k_k': 0.85 + normal(14, (L, RW_WIDTH), 0.02),
        'rw_k_a': 1.0 + normal(15, (L, RW_WIDTH), 0.02),
        'rw_r_k': normal(16, (L, RW_HEADS, RW_HEAD_DIM), 0.1),
        'rw_ln_w': 1.0 + normal(17, (L, RW_WIDTH), 0.02),
        'rw_ln_b': normal(18, (L, RW_WIDTH), 0.02),
        'w_out': normal(19, (L, D, D), D ** -0.5),
        'xa_norm_w': 1.0 + normal(20, (L, D), 0.02),
        'mem_norm_w': 1.0 + normal(21, (L, D), 0.02),
        'xa_wq': normal(22, (L, D, XA_WIDTH), D ** -0.5),
        'xa_wk': normal(23, (L, D, XA_WIDTH), D ** -0.5),
        'xa_wv': normal(24, (L, D, XA_WIDTH), D ** -0.5),
        'xa_wo': normal(25, (L, XA_WIDTH, D), XA_WIDTH ** -0.5),
        'ffn_norm_w': 1.0 + normal(26, (L, D), 0.02),
        'ffn_w1': normal(27, (L, D, FFN_HIDDEN), D ** -0.5),
        'ffn_w2': normal(28, (L, FFN_HIDDEN, D), FFN_HIDDEN ** -0.5),
        'final_norm_w': 1.0 + normal(29, (D,), 0.02),
    }


def reference(x, mem, mix_norm_w, w_in, dn_conv_w, dn_a_log, dn_dt_bias, dn_norm_w, rw_mu, rw_w0,
              rw_w2, rw_a0, rw_a2, rw_g2, rw_k_k, rw_k_a, rw_r_k, rw_ln_w, rw_ln_b, w_out,
              xa_norm_w, mem_norm_w, xa_wq, xa_wk, xa_wv, xa_wo, ffn_norm_w, ffn_w1, ffn_w2,
              final_norm_w):
    h = x
    for l in range(DEPTH):
        u = rmsnorm(h, mix_norm_w[l])
        p = u @ w_in[l]
        o_dn = deltanet_group(p[..., :DN_COLS], dn_conv_w[l], dn_a_log[l], dn_dt_bias[l], dn_norm_w[l])
        o_rw = rwkv7_group(p[..., DN_COLS:], rw_mu[l], rw_w0[l], rw_w2[l], rw_a0[l], rw_a2[l], rw_g2[l],
                           rw_k_k[l], rw_k_a[l], rw_r_k[l], rw_ln_w[l], rw_ln_b[l])
        h = h + jnp.concatenate([o_dn, o_rw], axis=-1).astype(h.dtype) @ w_out[l]
        h = h + memory_cross_attention(rmsnorm(h, xa_norm_w[l]), rmsnorm(mem, mem_norm_w[l]),
                                       xa_wq[l], xa_wk[l], xa_wv[l], xa_wo[l])
        h = h + squared_relu_mlp(rmsnorm(h, ffn_norm_w[l]), ffn_w1[l], ffn_w2[l])
    return rmsnorm(h, final_norm_w)
```

```python
import functools

import jax
import jax.numpy as jnp
from jax import lax
from jax.experimental import pallas as pl
from jax.experimental.pallas import tpu as pltpu

F32 = jnp.float32
BF16 = jnp.bfloat16

LANES = 128
SUBLANES = 8
VMEM_LIMIT_BYTES = 56 * 1024 * 1024

RMS_EPS = 1e-6
DN_HEAD_DIM = 128
DN_CONV = 4
DN_CHUNK = 128
RW_HEAD_DIM = 64
RW_CHUNK = 64
RW_GN_EPS = 64e-5
XA_HEADS = 4
XA_HEAD_DIM = 128


def _mm(a, b):
    return jnp.dot(a.astype(BF16), b.astype(BF16), preferred_element_type=F32)


def _mm_nt(a, b):
    return lax.dot_general(a.astype(BF16), b.astype(BF16), (((1,), (1,)), ((), ())),
                           preferred_element_type=F32)


def _softplus(x):
    return jnp.maximum(x, 0.0) + jnp.log(1.0 + jnp.exp(-jnp.abs(x)))


def _silu(x):
    return x * jax.nn.sigmoid(x)


def _shift_rows(cur, halo, j):
    rc = pltpu.roll(cur, j, 0)
    rh = pltpu.roll(halo, j, 0)
    sub = lax.broadcasted_iota(jnp.int32, halo.shape, 0)
    first = jnp.where(sub < j, rh, rc[:SUBLANES])
    return jnp.concatenate([first, rc[SUBLANES:]], axis=0)


def _cumsum_rows(x):
    n = x.shape[0]
    row = lax.broadcasted_iota(jnp.int32, x.shape, 0)
    s = 1
    while s < n:
        x = x + jnp.where(row >= s, pltpu.roll(x, s, 0), 0.0)
        s *= 2
    return x


def _unit_lower_inverse(a, top):
    n = a.shape[0]
    ii = lax.broadcasted_iota(jnp.int32, (n, n), 0)
    jj = lax.broadcasted_iota(jnp.int32, (n, n), 1)
    b1 = jnp.where((ii >> 3) == (jj >> 3), -a, 0.0)
    b2 = _mm(b1, b1)
    b4 = _mm(b2, b2)
    t = jnp.where(ii == jj, 1.0, 0.0) + b1
    t = t + _mm(b2, t)
    t = t + _mm(b4, t)
    sh = 3
    while (1 << sh) < top:
        off = jnp.where(((ii >> (sh + 1)) == (jj >> (sh + 1))) & ((ii >> sh) != (jj >> sh)), a, 0.0)
        t = t - _mm(t, _mm(off, t))
        sh += 1
    return t


def _norm_matmul_kernel(x_ref, nw_ref, w_ref, o_ref, u_ref):
    @pl.when(pl.program_id(1) == 0)
    def _():
        x = x_ref[...]
        ms = jnp.mean(x * x, axis=-1, keepdims=True)
        u_ref[...] = (x * lax.rsqrt(ms + RMS_EPS) * nw_ref[...]).astype(BF16)

    o_ref[...] = jnp.dot(u_ref[...], w_ref[...], preferred_element_type=F32).astype(o_ref.dtype)


def _norm_matmul(x, norm_w, w, *, tm, tn, out_dtype):
    rows, d = x.shape
    n = w.shape[1]
    return pl.pallas_call(
        _norm_matmul_kernel,
        grid=(rows // tm, n // tn),
        in_specs=[
            pl.BlockSpec((tm, d), lambda i, j: (i, 0)),
            pl.BlockSpec((1, d), lambda i, j: (0, 0)),
            pl.BlockSpec((d, tn), lambda i, j: (0, j)),
        ],
        out_specs=pl.BlockSpec((tm, tn), lambda i, j: (i, j)),
        out_shape=jax.ShapeDtypeStruct((rows, n), out_dtype),
        scratch_shapes=[pltpu.VMEM((tm, d), BF16)],
        compiler_params=pltpu.CompilerParams(
            dimension_semantics=("parallel", "arbitrary"), vmem_limit_bytes=VMEM_LIMIT_BYTES),
        name="norm_matmul",
    )(x, norm_w.reshape(1, d), w)


def _deltanet_kernel(q_ref, k_ref, v_ref, z_ref, ab_ref, cwq_ref, cwk_ref, cwv_ref, alog_ref, dtb_ref,
                     nw_ref, o_ref, st_ref, *, seq, heads):
    h = pl.program_id(1)
    c = DN_CHUNK
    st_ref[...] = jnp.zeros_like(st_ref)
    neg_a = -jnp.exp(alog_ref[...])
    dtb = dtb_ref[...]
    nw = nw_ref[...]
    scale = DN_HEAD_DIM ** -0.5

    def conv_silu(cur, halo, w_ref):
        acc = cur * w_ref[DN_CONV - 1:DN_CONV, :]
        for j in range(1, DN_CONV):
            acc = acc + _shift_rows(cur, halo, j) * w_ref[DN_CONV - 1 - j:DN_CONV - j, :]
        return _silu(acc)

    def body(n, carry):
        hq, hk, hv = carry
        r0 = pl.multiple_of(n * c, c)
        qr = q_ref[pl.ds(r0, c), :]
        kr = k_ref[pl.ds(r0, c), :]
        vr = v_ref[pl.ds(r0, c), :]
        q = conv_silu(qr, hq, cwq_ref)
        k = conv_silu(kr, hk, cwk_ref)
        v = conv_silu(vr, hv, cwv_ref)
        q = q * (lax.rsqrt(jnp.sum(q * q, axis=-1, keepdims=True) + 1e-6) * scale)
        k = k * lax.rsqrt(jnp.sum(k * k, axis=-1, keepdims=True) + 1e-6)

        lane = lax.broadcasted_iota(jnp.int32, (c, LANES), 1)
        ii = lax.broadcasted_iota(jnp.int32, (c, c), 0)
        jj = lax.broadcasted_iota(jnp.int32, (c, c), 1)
        ab = ab_ref[pl.ds(r0, c), :]
        gc_all = _cumsum_rows(neg_a * _softplus(ab + dtb))
        gc = jnp.sum(jnp.where(lane == h, gc_all, 0.0), axis=-1, keepdims=True)
        beta = jnp.sum(jnp.where(lane == h + heads, jax.nn.sigmoid(ab), 0.0), axis=-1, keepdims=True)
        gc_row = jnp.sum(jnp.where(ii == jj, gc, 0.0), axis=0, keepdims=True)
        g_last = jnp.sum(jnp.where(ii[:, :1] == c - 1, gc, 0.0), axis=0, keepdims=True)
        causal = ii >= jj
        dec = jnp.where(causal, jnp.exp(jnp.where(causal, gc - gc_row, 0.0)), 0.0)
        egc = jnp.exp(gc)

        kb = k * beta
        kq = _mm_nt(jnp.concatenate([kb, q], axis=0), k)
        a = jnp.where(ii > jj, kq[:c] * dec, 0.0)
        attn = kq[c:] * dec
        t = _unit_lower_inverse(a, c)
        uw = _mm(t, jnp.concatenate([v * beta, kb * egc], axis=1))
        s = st_ref[...]
        ws_qs = _mm(jnp.concatenate([uw[:, LANES:], q * egc], axis=0), s)
        v_new = uw[:, :LANES] - ws_qs[:c]
        o = ws_qs[c:] + _mm(attn, v_new)
        kd = k * jnp.exp(g_last - gc)
        st_ref[...] = s * jnp.exp(g_last) + _mm(kd.T, v_new)

        o = o * lax.rsqrt(jnp.mean(o * o, axis=-1, keepdims=True) + RMS_EPS) * nw
        o = o * _silu(z_ref[pl.ds(r0, c), :])
        o_ref[pl.ds(r0, c), :] = o.astype(o_ref.dtype)
        return qr[c - SUBLANES:], kr[c - SUBLANES:], vr[c - SUBLANES:]

    zero = jnp.zeros((SUBLANES, LANES), F32)
    lax.fori_loop(0, seq // c, body, (zero, zero, zero))


def _deltanet(p, conv_w, alog_row, dtb_row, norm_w, *, batch, seq, heads, col):
    blk = lambda off: pl.BlockSpec((seq, LANES), lambda b, h: (b, off + h))
    cw = lambda off: pl.BlockSpec((DN_CONV, LANES), lambda b, h: (0, off + h))
    row = pl.BlockSpec((1, LANES), lambda b, h: (0, 0))
    return pl.pallas_call(
        functools.partial(_deltanet_kernel, seq=seq, heads=heads),
        grid=(batch, heads),
        in_specs=[blk(col["q"]), blk(col["k"]), blk(col["v"]), blk(col["z"]),
                  pl.BlockSpec((seq, LANES), lambda b, h: (b, col["ab"])),
                  cw(0), cw(heads), cw(2 * heads), row, row, row],
        out_specs=pl.BlockSpec((seq, LANES), lambda b, h: (b, h)),
        out_shape=jax.ShapeDtypeStruct((batch * seq, heads * DN_HEAD_DIM), BF16),
        scratch_shapes=[pltpu.VMEM((DN_HEAD_DIM, DN_HEAD_DIM), F32)],
        compiler_params=pltpu.CompilerParams(
            dimension_semantics=("parallel", "parallel"), vmem_limit_bytes=VMEM_LIMIT_BYTES),
        name="deltanet",
    )(p, p, p, p, p, conv_w, conv_w, conv_w, alog_row, dtb_row, norm_w)


def _rwkv7_kernel(r_ref, k_ref, v_ref, lo_ref, mur_ref, muk_ref, muv_ref, mulo_ref, w0_ref, w2_ref, a0_ref,
                  a2_ref, g2_ref, kk_ref, ka_ref, rk_ref, lnw_ref, lnb_ref, o_ref,
                  rs_ref, lw_ref, ks_ref, vs_ref, nn_ref, pp_ref, gate_ref, bonus_ref, y_ref, st_ref,
                  *, seq, tile):
    c = RW_CHUNK
    c2 = 2 * c
    li = lax.broadcasted_iota(jnp.int32, (LANES, LANES), 0)
    lj = lax.broadcasted_iota(jnp.int32, (LANES, LANES), 1)
    head_ones = jnp.where((li >> 6) == (lj >> 6), 1.0, 0.0).astype(BF16)

    def head_sum(x):
        return jnp.dot(x.astype(BF16), head_ones, preferred_element_type=F32)

    def prep(i, carry):
        hr, hk, hv, hlo = carry
        r0 = pl.multiple_of(i * tile, tile)
        rr = r_ref[pl.ds(r0, tile), :]
        kr = k_ref[pl.ds(r0, tile), :]
        vr = v_ref[pl.ds(r0, tile), :]
        lor = lo_ref[pl.ds(r0, tile), :]

        def lerp(cur, halo, mu_ref):
            return cur + (_shift_rows(cur, halo, 1) - cur) * mu_ref[...]

        r = lerp(rr, hr, mur_ref)
        k = lerp(kr, hk, muk_ref)
        v = lerp(vr, hv, muv_ref)
        lo = lerp(lor, hlo, mulo_ref)
        lo_wa = lo[:, :LANES]
        log_w = -_softplus(-(w0_ref[...] + _mm(jnp.tanh(lo_wa), w2_ref[...]))) - 0.5
        a = jax.nn.sigmoid(a0_ref[...] + _mm(lo_wa, a2_ref[...]))
        gate = _mm(jax.nn.sigmoid(lo[:, LANES:]), g2_ref[...])
        kk = k * kk_ref[...]
        kk = kk / jnp.maximum(jnp.sqrt(head_sum(kk * kk)), 1e-12)
        k2 = k * (1.0 + (a - 1.0) * ka_ref[...])
        rs_ref[pl.ds(r0, tile), :] = r
        lw_ref[pl.ds(r0, tile), :] = -jnp.exp(log_w)
        ks_ref[pl.ds(r0, tile), :] = k2
        vs_ref[pl.ds(r0, tile), :] = v
        nn_ref[pl.ds(r0, tile), :] = kk
        pp_ref[pl.ds(r0, tile), :] = -(a * kk)
        gate_ref[pl.ds(r0, tile), :] = gate
        bonus_ref[pl.ds(r0, tile), :] = head_sum(r * k2 * rk_ref[...]) * v
        t8 = tile - SUBLANES
        return rr[t8:], kr[t8:], vr[t8:], lor[t8:]

    z1 = jnp.zeros((SUBLANES, LANES), F32)
    lax.fori_loop(0, seq // tile, prep, (z1, z1, z1, jnp.zeros((SUBLANES, 2 * LANES), F32)))

    st_ref[...] = jnp.zeros_like(st_ref)

    def chunk(n, _):
        r0 = pl.multiple_of(n * c, c)
        r = rs_ref[pl.ds(r0, c), :]
        lw = lw_ref[pl.ds(r0, c), :]
        k = ks_ref[pl.ds(r0, c), :]
        v = vs_ref[pl.ds(r0, c), :]
        nn = nn_ref[pl.ds(r0, c), :]
        pp = pp_ref[pl.ds(r0, c), :]

        lane = lax.broadcasted_iota(jnp.int32, (c, LANES), 1)
        m0 = jnp.where(lane < RW_HEAD_DIM, 1.0, 0.0)
        m1 = 1.0 - m0

        def stack(x):
            return jnp.concatenate([x * m0, x * m1], axis=0)

        def dup(x):
            return jnp.concatenate([x, x], axis=0)

        row = lax.broadcasted_iota(jnp.int32, (c, LANES), 0)
        cs = _cumsum_rows(lw)
        cs_last = jnp.sum(jnp.where(row == c - 1, cs, 0.0), axis=0, keepdims=True)
        g_inv = jnp.exp(-cs)
        g_to_end = jnp.exp(cs_last - cs)
        n_s = stack(nn * jnp.exp(cs - lw))
        r_s = stack(r * jnp.exp(cs))
        v_s = stack(v)

        ii = lax.broadcasted_iota(jnp.int32, (c2, c2), 0)
        jj = lax.broadcasted_iota(jnp.int32, (c2, c2), 1)
        same = (ii >> 6) == (jj >> 6)
        strict = same & (ii > jj)
        incl = same & (ii >= jj)
        pw = _mm_nt(jnp.concatenate([n_s, r_s], axis=0),
                    jnp.concatenate([dup(pp * g_inv), dup(k * g_inv)], axis=0))
        a_np = jnp.where(strict, pw[:c2, :c2], 0.0)
        a_nk = jnp.where(strict, pw[:c2, c2:], 0.0)
        a_rp = jnp.where(incl, pw[c2:, :c2], 0.0)
        a_rk = jnp.where(incl, pw[c2:, c2:], 0.0)

        t = _unit_lower_inverse(-a_np, c)
        wu = _mm(t, jnp.concatenate([n_s, _mm(a_nk, v_s)], axis=1))
        s = st_ref[...]
        ws_rs = _mm_nt(jnp.concatenate([wu[:, :LANES], r_s], axis=0), s)
        d = ws_rs[:c2] + wu[:, LANES:]
        dv = jnp.concatenate([d, v_s], axis=0)
        y_s = ws_rs[c2:] + _mm(jnp.concatenate([a_rp, a_rk], axis=1), dv)
        y_ref[pl.ds(r0, c), :] = y_s[:c] + y_s[c:]
        pk_end = jnp.concatenate([stack(pp * g_to_end), stack(k * g_to_end)], axis=0)
        st_ref[...] = s * jnp.exp(cs_last) + _mm(dv.T, pk_end)
        return 0

    lax.fori_loop(0, seq // c, chunk, 0)

    def finish(i, _):
        r0 = pl.multiple_of(i * tile, tile)
        y = y_ref[pl.ds(r0, tile), :]
        dlt = y - head_sum(y) * (1.0 / RW_HEAD_DIM)
        var = head_sum(dlt * dlt) * (1.0 / RW_HEAD_DIM)
        yn = dlt * lax.rsqrt(var + RW_GN_EPS) * lnw_ref[...] + lnb_ref[...]
        o_ref[pl.ds(r0, tile), :] = ((yn + bonus_ref[pl.ds(r0, tile), :])
                                     * gate_ref[pl.ds(r0, tile), :]).astype(o_ref.dtype)
        return 0

    lax.fori_loop(0, seq // tile, finish, 0)


def _rwkv7(p, rows, w2p, a2p, g2, *, batch, seq, pairs, col, tile):
    blk = lambda off: pl.BlockSpec((seq, LANES), lambda b, h: (b, off + h))
    prow = lambda off: pl.BlockSpec((1, LANES), lambda b, h: (0, off + h))
    wcol = lambda nrows: pl.BlockSpec((nrows, LANES), lambda b, h: (0, h))
    seq_buf = pltpu.VMEM((seq, LANES), F32)
    return pl.pallas_call(
        functools.partial(_rwkv7_kernel, seq=seq, tile=tile),
        grid=(batch, pairs),
        in_specs=[blk(col["r"]), blk(col["k"]), blk(col["v"]),
                  pl.BlockSpec((seq, 2 * LANES), lambda b, h: (b, col["lora"] // 2)),
                  prow(0), prow(pairs), prow(2 * pairs),
                  pl.BlockSpec((1, 2 * LANES), lambda b, h: (0, (3 * pairs) // 2)),
                  prow(0), wcol(LANES), prow(0), wcol(LANES), wcol(LANES),
                  prow(0), prow(0), prow(0), prow(0), prow(0)],
        out_specs=pl.BlockSpec((seq, LANES), lambda b, h: (b, h)),
        out_shape=jax.ShapeDtypeStruct((batch * seq, pairs * LANES), BF16),
        scratch_shapes=[seq_buf] * 9 + [pltpu.VMEM((LANES, LANES), F32)],
        compiler_params=pltpu.CompilerParams(
            dimension_semantics=("parallel", "parallel"), vmem_limit_bytes=VMEM_LIMIT_BYTES),
        name="rwkv7",
    )(p, p, p, p, rows["mu"], rows["mu"], rows["mu"], rows["mu"], rows["w0"], w2p, rows["a0"], a2p, g2,
      rows["k_k"], rows["k_a"], rows["r_k"], rows["ln_w"], rows["ln_b"])


def _mid_kernel(x_ref, odn_ref, orw_ref, wout_ref, nw_ref, wq_ref, kv_ref, wo_ref, h_ref, *, dn_width):
    mix = (jnp.dot(odn_ref[...], wout_ref[:dn_width, :], preferred_element_type=F32)
           + jnp.dot(orw_ref[...], wout_ref[dn_width:, :], preferred_element_type=F32))
    h1 = x_ref[...] + mix
    hn = h1 * lax.rsqrt(jnp.mean(h1 * h1, axis=-1, keepdims=True) + RMS_EPS) * nw_ref[...]
    q = jnp.dot(hn.astype(BF16), wq_ref[...], preferred_element_type=F32).astype(BF16)
    xa_width = XA_HEADS * XA_HEAD_DIM
    outs = []
    for hd in range(XA_HEADS):
        sl = slice(hd * XA_HEAD_DIM, (hd + 1) * XA_HEAD_DIM)
        s = _mm_nt(q[:, sl], kv_ref[:, sl]) * (XA_HEAD_DIM ** -0.5)
        e = jnp.exp(s - jnp.max(s, axis=-1, keepdims=True))
        den = jnp.sum(e, axis=-1, keepdims=True)
        pv = jnp.dot(e.astype(BF16), kv_ref[:, xa_width + hd * XA_HEAD_DIM:xa_width + (hd + 1) * XA_HEAD_DIM],
                     preferred_element_type=F32)
        outs.append(pv / den)
    o = jnp.concatenate(outs, axis=1).astype(BF16)
    h_ref[...] = h1 + jnp.dot(o, wo_ref[...], preferred_element_type=F32)


def _mid(x, o_dn, o_rw, w_out, xa_norm_w, wq, kv, wo, *, seq, mem_len, tm):
    rows, d = x.shape
    dn_width = o_dn.shape[1]
    rw_width = o_rw.shape[1]
    xa_width = wq.shape[1]
    per_batch = seq // tm
    const = lambda shape: pl.BlockSpec(shape, lambda i: (0, 0))
    return pl.pallas_call(
        functools.partial(_mid_kernel, dn_width=dn_width),
        grid=(rows // tm,),
        in_specs=[
            pl.BlockSpec((tm, d), lambda i: (i, 0)),
            pl.BlockSpec((tm, dn_width), lambda i: (i, 0)),
            pl.BlockSpec((tm, rw_width), lambda i: (i, 0)),
            const((dn_width + rw_width, d)),
            const((1, d)),
            const((d, xa_width)),
            pl.BlockSpec((mem_len, 2 * xa_width), lambda i: (i // per_batch, 0)),
            const((xa_width, d)),
        ],
        out_specs=pl.BlockSpec((tm, d), lambda i: (i, 0)),
        out_shape=jax.ShapeDtypeStruct((rows, d), F32),
        compiler_params=pltpu.CompilerParams(
            dimension_semantics=("parallel",), vmem_limit_bytes=VMEM_LIMIT_BYTES),
        name="mid",
    )(x, o_dn, o_rw, w_out, xa_norm_w.reshape(1, d), wq, kv, wo)


def _ffn_kernel(h_ref, nw_ref, w1_ref, w2_ref, fnw_ref, o_ref, u_ref, acc_ref):
    f = pl.program_id(1)

    @pl.when(f == 0)
    def _():
        h = h_ref[...]
        u_ref[...] = (h * lax.rsqrt(jnp.mean(h * h, axis=-1, keepdims=True) + RMS_EPS) * nw_ref[...]).astype(BF16)
        acc_ref[...] = jnp.zeros_like(acc_ref)

    a = jnp.maximum(jnp.dot(u_ref[...], w1_ref[...], preferred_element_type=F32), 0.0)
    acc_ref[...] += jnp.dot((a * a).astype(BF16), w2_ref[...], preferred_element_type=F32)

    @pl.when(f == pl.num_programs(1) - 1)
    def _():
        y = h_ref[...] + acc_ref[...]
        o_ref[...] = y * lax.rsqrt(jnp.mean(y * y, axis=-1, keepdims=True) + RMS_EPS) * fnw_ref[...]


def _ffn(h, ffn_norm_w, w1, w2, final_norm_w, *, tm, tf):
    rows, d = h.shape
    hidden = w1.shape[1]
    return pl.pallas_call(
        _ffn_kernel,
        grid=(rows // tm, hidden // tf),
        in_specs=[
            pl.BlockSpec((tm, d), lambda i, f: (i, 0)),
            pl.BlockSpec((1, d), lambda i, f: (0, 0)),
            pl.BlockSpec((d, tf), lambda i, f: (0, f)),
            pl.BlockSpec((tf, d), lambda i, f: (f, 0)),
            pl.BlockSpec((1, d), lambda i, f: (0, 0)),
        ],
        out_specs=pl.BlockSpec((tm, d), lambda i, f: (i, 0)),
        out_shape=jax.ShapeDtypeStruct((rows, d), F32),
        scratch_shapes=[pltpu.VMEM((tm, d), BF16), pltpu.VMEM((tm, d), F32)],
        compiler_params=pltpu.CompilerParams(
            dimension_semantics=("parallel", "arbitrary"), vmem_limit_bytes=VMEM_LIMIT_BYTES),
        name="ffn",
    )(h, ffn_norm_w.reshape(1, d), w1, w2, final_norm_w.reshape(1, d))


def _pick_tile(n, want):
    t = min(n, want)
    while n % t:
        t //= 2
    return t


def _layer(h, mem, mix_norm_w, w_in, dn_conv_w, dn_a_log, dn_dt_bias, dn_norm_w, rw_mu, rw_w0, rw_w2, rw_a0,
           rw_a2, rw_g2, rw_k_k, rw_k_a, rw_r_k, rw_ln_w, rw_ln_b, w_out, xa_norm_w, mem_norm_w, xa_wq, xa_wk,
           xa_wv, xa_wo, ffn_norm_w, ffn_w1, ffn_w2, out_norm_w, *, batch, seq, mem_len):
    d = h.shape[1]
    dn_heads = dn_a_log.shape[0]
    dn_width = dn_heads * DN_HEAD_DIM
    rw_width = rw_w0.shape[0]
    pairs = rw_width // LANES
    decay_lora, rate_lora, gate_lora = rw_w2.shape[0], rw_a2.shape[0], rw_g2.shape[0]
    assert decay_lora + rate_lora == LANES and gate_lora == LANES
    assert 2 * dn_heads <= LANES and pairs % 2 == 0 and dn_heads % 2 == 0
    dn_cols = 4 * dn_width + 2 * dn_heads
    in_cols = w_in.shape[1]

    nblk = -(-(in_cols - 2 * dn_heads) // LANES) + 1
    tn = 512
    n_pad = -(-(nblk * LANES) // tn) * tn
    w_perm = jnp.concatenate(
        [w_in[:, :4 * dn_width], w_in[:, dn_cols:], w_in[:, 4 * dn_width:dn_cols],
         jnp.zeros((d, n_pad - in_cols), w_in.dtype)], axis=1).astype(BF16)
    hb = dn_width // LANES
    col = {"q": 0, "k": hb, "v": 2 * hb, "z": 3 * hb}
    rcol = {"r": 4 * hb, "k": 4 * hb + pairs, "v": 4 * hb + 2 * pairs, "lora": 4 * hb + 3 * pairs}
    col["ab"] = rcol["lora"] + 2

    rows = batch * seq
    p = _norm_matmul(h, mix_norm_w, w_perm, tm=_pick_tile(rows, 512), tn=tn, out_dtype=F32)

    lane_row = lambda vec: jnp.zeros((1, LANES), F32).at[0, :vec.shape[0]].set(vec)
    o_dn = _deltanet(p, dn_conv_w, lane_row(dn_a_log), lane_row(dn_dt_bias), dn_norm_w.reshape(1, LANES),
                     batch=batch, seq=seq, heads=dn_heads, col=col)

    prm = {"mu": rw_mu, "w0": rw_w0, "a0": rw_a0, "k_k": rw_k_k, "k_a": rw_k_a, "r_k": rw_r_k.reshape(-1),
           "ln_w": rw_ln_w, "ln_b": rw_ln_b}
    prm = {name: val.reshape(1, -1) for name, val in prm.items()}
    w2p = jnp.concatenate([rw_w2, jnp.zeros((rate_lora, rw_width), F32)], axis=0)
    a2p = jnp.concatenate([jnp.zeros((decay_lora, rw_width), F32), rw_a2], axis=0)
    o_rw = _rwkv7(p, prm, w2p, a2p, rw_g2, batch=batch, seq=seq, pairs=pairs, col=rcol,
                  tile=_pick_tile(seq, 256))

    kv = _norm_matmul(mem, mem_norm_w, jnp.concatenate([xa_wk, xa_wv], axis=1).astype(BF16),
                      tm=_pick_tile(batch * mem_len, 512), tn=512, out_dtype=BF16)
    h2 = _mid(h, o_dn, o_rw, w_out.astype(BF16), xa_norm_w, xa_wq.astype(BF16), kv, xa_wo.astype(BF16),
              seq=seq, mem_len=mem_len, tm=_pick_tile(seq, 256))
    return _ffn(h2, ffn_norm_w, ffn_w1.astype(BF16), ffn_w2.astype(BF16), out_norm_w,
                tm=_pick_tile(rows, 512), tf=512)


def kernel(x, mem, mix_norm_w, w_in, dn_conv_w, dn_a_log, dn_dt_bias, dn_norm_w, rw_mu, rw_w0, rw_w2, rw_a0, rw_a2, rw_g2, rw_k_k, rw_k_a, rw_r_k, rw_ln_w, rw_ln_b, w_out, xa_norm_w, mem_norm_w, xa_wq, xa_wk, xa_wv, xa_wo, ffn_norm_w, ffn_w1, ffn_w2, final_norm_w):
    batch, seq, d = x.shape
    mem_len = mem.shape[1]
    depth = w_in.shape[0]
    assert depth == 1, "the fused final norm assumes a single layer"
    h = x.reshape(batch * seq, d)
    m = mem.reshape(batch * mem_len, d)
    out = _layer(h, m, mix_norm_w[0], w_in[0], dn_conv_w[0], dn_a_log[0], dn_dt_bias[0], dn_norm_w[0], rw_mu[0],
                 rw_w0[0], rw_w2[0], rw_a0[0], rw_a2[0], rw_g2[0], rw_k_k[0], rw_k_a[0], rw_r_k[0], rw_ln_w[0],
                 rw_ln_b[0], w_out[0], xa_norm_w[0], mem_norm_w[0], xa_wq[0], xa_wk[0], xa_wv[0], xa_wo[0],
                 ffn_norm_w[0], ffn_w1[0], ffn_w2[0], final_norm_w, batch=batch, seq=seq, mem_len=mem_len)
    return out.reshape(batch, seq, d)
```
